```python
import math
import jax, jax.numpy as jnp
from jax import lax
import numpy as np

D_MODEL = 1024
BATCH = 2
SEQ = 16384
DEPTH = 4

N_MIXERS = 2
N_MLA = (DEPTH + 1) // 2
N_LRU = DEPTH // 2

MLA_HEADS = 8
QK_NOPE_DIM = 128
QK_ROPE_DIM = 64
V_HEAD_DIM = 128
Q_LORA_RANK = 384
KV_LORA_RANK = 256
ROPE_THETA = 10000.0
Q_BLOCK = 128
ATTN_SCALE = 1.0 / math.sqrt(QK_NOPE_DIM + QK_ROPE_DIM)

LRU_WIDTH = D_MODEL
LRU_BLOCKS = 8
LRU_BLOCK_SIZE = LRU_WIDTH // LRU_BLOCKS
CONV_WIDTH = 4
LRU_C = 8.0

FFN_HIDDEN = 4 * D_MODEL
NORM_EPS = 1e-6

kernel_name = "hybrid_mla_rglru_sandwich_trunk"


def _rmsnorm(x, g):
    x32 = x.astype(jnp.float32)
    y = x32 * lax.rsqrt(jnp.mean(x32 * x32, axis=-1, keepdims=True) + NORM_EPS)
    return (y * g.astype(jnp.float32)).astype(x.dtype)


def _rotary(x, cos, sin):
    half = x.shape[-1] // 2
    x1, x2 = x[..., :half], x[..., half:]
    return jnp.concatenate([x1 * cos - x2 * sin, x2 * cos + x1 * sin], axis=-1)


def _mla(h, positions, w_in, q_norm, kv_norm, w_uq, w_ukv, w_o):
    B, S, _ = h.shape
    H = MLA_HEADS
    proj = h @ w_in
    c_q = _rmsnorm(proj[..., :Q_LORA_RANK], q_norm)
    c_kv = _rmsnorm(proj[..., Q_LORA_RANK:Q_LORA_RANK + KV_LORA_RANK], kv_norm)
    k_rope = proj[..., Q_LORA_RANK + KV_LORA_RANK:]
    q = (c_q @ w_uq).reshape(B, S, H, QK_NOPE_DIM + QK_ROPE_DIM)
    q_nope, q_rope = q[..., :QK_NOPE_DIM], q[..., QK_NOPE_DIM:]
    kv = (c_kv @ w_ukv).reshape(B, S, H, QK_NOPE_DIM + V_HEAD_DIM)
    k_nope, v = kv[..., :QK_NOPE_DIM], kv[..., QK_NOPE_DIM:]

    inv_freq = ROPE_THETA ** (-jnp.arange(0, QK_ROPE_DIM, 2, dtype=jnp.float32) / QK_ROPE_DIM)
    ang = positions.astype(jnp.float32)[..., None] * inv_freq
    cos, sin = jnp.cos(ang).astype(h.dtype), jnp.sin(ang).astype(h.dtype)
    q_rope = _rotary(q_rope, cos[:, :, None, :], sin[:, :, None, :])
    k_rope = _rotary(k_rope, cos, sin)

    nqb = S // Q_BLOCK
    qn_b = q_nope.reshape(B, nqb, Q_BLOCK, H, QK_NOPE_DIM).transpose(1, 0, 2, 3, 4)
    qr_b = q_rope.reshape(B, nqb, Q_BLOCK, H, QK_ROPE_DIM).transpose(1, 0, 2, 3, 4)
    kpos = jnp.arange(S)

    def attend(args):
        qn, qr, blk = args
        s = (jnp.einsum('bqhd,bkhd->bhqk', qn, k_nope)
             + jnp.einsum('bqhr,bkr->bhqk', qr, k_rope)).astype(jnp.float32) * ATTN_SCALE
        qpos = blk * Q_BLOCK + jnp.arange(Q_BLOCK)
        s = jnp.where(kpos[None, :] <= qpos[:, None], s, -jnp.inf)
        p = jax.nn.softmax(s, axis=-1).astype(v.dtype)
        return jnp.einsum('bhqk,bkhd->bqhd', p, v)

    o = lax.map(attend, (qn_b, qr_b, jnp.arange(nqb)))
    o = o.transpose(1, 0, 2, 3, 4).reshape(B, S, H * V_HEAD_DIM)
    return o @ w_o


def _rglru_block(h, w_in, conv_w, conv_b, w_a, b_a, w_x, b_x, lam, w_out):
    B, S, _ = h.shape
    proj = h @ w_in
    gate = jax.nn.gelu(proj[..., :LRU_WIDTH], approximate=True)
    rec = proj[..., LRU_WIDTH:]
    xp = jnp.pad(rec, ((0, 0), (CONV_WIDTH - 1, 0), (0, 0)))
    xc = conv_b + sum(xp[:, k:k + S] * conv_w[k] for k in range(CONV_WIDTH))
    xc32 = xc.astype(jnp.float32)
    xb = xc32.reshape(B, S, LRU_BLOCKS, LRU_BLOCK_SIZE)
    r = jax.nn.sigmoid(jnp.einsum('bsnc,ncd->bsnd', xb, w_a.astype(jnp.float32)).reshape(B, S, LRU_WIDTH)
                       + b_a.astype(jnp.float32))
    i = jax.nn.sigmoid(jnp.einsum('bsnc,ncd->bsnd', xb, w_x.astype(jnp.float32)).reshape(B, S, LRU_WIDTH)
                       + b_x.astype(jnp.float32))
    log_a = -LRU_C * r * jax.nn.softplus(-lam.astype(jnp.float32))
    a = jnp.exp(log_a)
    b = jnp.sqrt(-jnp.expm1(2.0 * log_a)) * (i * xc32)

    def combine(left, right):
        a1, b1 = left
        a2, b2 = right
        return a1 * a2, a2 * b1 + b2

    _, hs = lax.associative_scan(combine, (a, b), axis=1)
    return (gate * hs.astype(h.dtype)) @ w_out


def _sq_relu_mlp(h, w_up, w_down):
    u = jax.nn.relu(h @ w_up)
    return (u * u) @ w_down


def setup_inputs(seed: int = 0) -> dict:
    key = jax.random.key(seed)
    ks = jax.random.split(key, 24)
    f32 = jnp.float32

    def nrm(k, shape, fan_in):
        return jax.random.normal(k, shape, f32) * (fan_in ** -0.5)

    def gain(k, shape):
        return 1.0 + 0.05 * jax.random.normal(k, shape, f32)

    x = jax.random.normal(ks[0], (BATCH, SEQ, D_MODEL), f32)
    positions = jnp.broadcast_to(jnp.arange(SEQ, dtype=jnp.int32), (BATCH, SEQ))
    mla_in = Q_LORA_RANK + KV_LORA_RANK + QK_ROPE_DIM
    u = jax.random.uniform(ks[14], (N_LRU, LRU_WIDTH), f32, 0.9, 0.999)
    s0 = u ** (1.0 / LRU_C)
    lam = jnp.log(s0) - jnp.log1p(-s0)
    return {
        "x": x,
        "positions": positions,
        "mix_pre_norm": gain(ks[1], (DEPTH, D_MODEL)),
        "mix_post_norm": gain(ks[2], (DEPTH, D_MODEL)),
        "ffn_pre_norm": gain(ks[3], (DEPTH, D_MODEL)),
        "ffn_post_norm": gain(ks[4], (DEPTH, D_MODEL)),
        "mla_w_in": nrm(ks[5], (N_MLA, D_MODEL, mla_in), D_MODEL),
        "mla_q_norm": gain(ks[6], (N_MLA, Q_LORA_RANK)),
        "mla_kv_norm": gain(ks[7], (N_MLA, KV_LORA_RANK)),
        "mla_w_uq": nrm(ks[8], (N_MLA, Q_LORA_RANK, MLA_HEADS * (QK_NOPE_DIM + QK_ROPE_DIM)), Q_LORA_RANK),
        "mla_w_ukv": nrm(ks[9], (N_MLA, KV_LORA_RANK, MLA_HEADS * (QK_NOPE_DIM + V_HEAD_DIM)), KV_LORA_RANK),
        "mla_w_o": nrm(ks[10], (N_MLA, MLA_HEADS * V_HEAD_DIM, D_MODEL), MLA_HEADS * V_HEAD_DIM),
        "lru_w_in": nrm(ks[11], (N_LRU, D_MODEL, 2 * LRU_WIDTH), D_MODEL),
        "lru_conv_w": nrm(ks[12], (N_LRU, CONV_WIDTH, LRU_WIDTH), CONV_WIDTH),
        "lru_conv_b": 0.01 * jax.random.normal(ks[13], (N_LRU, LRU_WIDTH), f32),
        "lru_w_a": nrm(ks[15], (N_LRU, LRU_BLOCKS, LRU_BLOCK_SIZE, LRU_BLOCK_SIZE), LRU_BLOCK_SIZE),
        "lru_b_a": 0.01 * jax.random.normal(ks[16], (N_LRU, LRU_WIDTH), f32),
        "lru_w_x": nrm(ks[17], (N_LRU, LRU_BLOCKS, LRU_BLOCK_SIZE, LRU_BLOCK_SIZE), LRU_BLOCK_SIZE),
        "lru_b_x": 0.01 * jax.random.normal(ks[18], (N_LRU, LRU_WIDTH), f32),
        "lru_lambda": lam,
        "lru_w_out": nrm(ks[19], (N_LRU, LRU_WIDTH, D_MODEL), LRU_WIDTH),
        "ffn_w_up": nrm(ks[20], (DEPTH, D_MODEL, FFN_HIDDEN), D_MODEL),
        "ffn_w_down": nrm(ks[21], (DEPTH, FFN_HIDDEN, D_MODEL), FFN_HIDDEN),
    }


def reference(x, positions, mix_pre_norm, mix_post_norm, ffn_pre_norm, ffn_post_norm,
              mla_w_in, mla_q_norm, mla_kv_norm, mla_w_uq, mla_w_ukv, mla_w_o,
              lru_w_in, lru_conv_w, lru_conv_b, lru_w_a, lru_b_a, lru_w_x, lru_b_x,
              lru_lambda, lru_w_out, ffn_w_up, ffn_w_down):
    for layer in range(DEPTH):
        j = layer // N_MIXERS
        hn = _rmsnorm(x, mix_pre_norm[layer])
        if layer % N_MIXERS == 0:
            y = _mla(hn, positions, mla_w_in[j], mla_q_norm[j], mla_kv_norm[j],
                     mla_w_uq[j], mla_w_ukv[j], mla_w_o[j])
        else:
            y = _rglru_block(hn, lru_w_in[j], lru_conv_w[j], lru_conv_b[j], lru_w_a[j], lru_b_a[j],
                             lru_w_x[j], lru_b_x[j], lru_lambda[j], lru_w_out[j])
        x = x + _rmsnorm(y, mix_post_norm[layer])
        hn = _rmsnorm(x, ffn_pre_norm[layer])
        y = _sq_relu_mlp(hn, ffn_w_up[layer], ffn_w_down[layer])
        x = x + _rmsnorm(y, ffn_post_norm[layer])
    return x
```

```python
import functools
import math

import jax
import jax.numpy as jnp
from jax import lax
from jax.experimental import pallas as pl
from jax.experimental.pallas import tpu as pltpu

D_MODEL = 1024
MLA_HEADS = 8
QK_NOPE_DIM = 128
QK_ROPE_DIM = 64
QK_DIM = QK_NOPE_DIM + QK_ROPE_DIM
V_HEAD_DIM = 128
Q_LORA_RANK = 384
KV_LORA_RANK = 256
ROPE_THETA = 10000.0
LRU_WIDTH = D_MODEL
LRU_BLOCKS = 8
LRU_BLOCK_SIZE = LRU_WIDTH // LRU_BLOCKS
CONV_WIDTH = 4
LRU_C = 8.0
FFN_HIDDEN = 4 * D_MODEL
NORM_EPS = 1e-6

SCORE_SCALE = math.log2(math.e) / math.sqrt(QK_DIM)

SUBLANES = 8
Q_HEAD_ROWS = 256
MIB = 1024 * 1024

ROW_TILE = 512
ROPE_TILE = 2048
ATTN_Q_TILE = 512
ATTN_K_TILE = 512
FFN_CHUNK = 1024

BF16 = jnp.bfloat16
F32 = jnp.float32

_NT = (((1,), (1,)), ((), ()))


def _dot(a, b):
    return jnp.dot(a, b, preferred_element_type=F32)


def _dot_nt(a, b):
    return lax.dot_general(a, b, _NT, preferred_element_type=F32)


def _rms(x, g):
    return x * lax.rsqrt(jnp.mean(x * x, axis=-1, keepdims=True) + NORM_EPS) * g


def _resident(shape):
    zeros = (0,) * len(shape)
    return pl.BlockSpec(shape, lambda *_: zeros, pipeline_mode=pl.Buffered(1))


def _rope_table_kernel(pos_ref, inv_ref, cc_ref, ss_ref):
    ang = inv_ref[...] * pos_ref[...].astype(F32)
    c = jnp.cos(ang)
    s = jnp.sin(ang)
    cc_ref[...] = jnp.concatenate([c, c], axis=0)
    ss_ref[...] = jnp.concatenate([-s, s], axis=0)


def _rope_tables(positions, inv_freq):
    b, s = positions.shape
    half = QK_ROPE_DIM // 2
    out = jax.ShapeDtypeStruct((b, QK_ROPE_DIM, s), F32)
    return pl.pallas_call(
        _rope_table_kernel,
        grid=(b, s // ROPE_TILE),
        in_specs=[pl.BlockSpec((None, 1, ROPE_TILE), lambda i, j: (i, 0, j)),
                  pl.BlockSpec((half, 1), lambda i, j: (0, 0))],
        out_specs=[pl.BlockSpec((None, QK_ROPE_DIM, ROPE_TILE), lambda i, j: (i, 0, j))] * 2,
        out_shape=[out, out],
        name="rope_tables",
    )(positions.reshape(b, 1, s), inv_freq.reshape(half, 1))


def _mla_proj_kernel(x_ref, g_ref, w_in_ref, gq_ref, gkv_ref, w_uqt_ref, w_kn_ref, w_vt_ref,
                     cct_ref, sst_ref, cc_ref, ss_ref, qt_ref, k_ref, vt_ref):
    hn = _rms(x_ref[...], g_ref[...]).astype(BF16)
    proj = _dot(hn, w_in_ref[...])
    c_q = _rms(proj[:, :Q_LORA_RANK], gq_ref[...]).astype(BF16)
    kv_end = Q_LORA_RANK + KV_LORA_RANK
    c_kv = _rms(proj[:, Q_LORA_RANK:kv_end], gkv_ref[...]).astype(BF16)
    k_rope = proj[:, kv_end:kv_end + QK_ROPE_DIM]
    k_rope_swapped = proj[:, kv_end + QK_ROPE_DIM:]
    k_rot = (k_rope * cc_ref[...] + k_rope_swapped * ss_ref[...]).astype(BF16)

    qt = _dot_nt(w_uqt_ref[...], c_q)
    cct = cct_ref[...]
    sst = sst_ref[...]
    for h in range(MLA_HEADS):
        base = h * Q_HEAD_ROWS
        q_nope = qt[base:base + QK_NOPE_DIM]
        q_rope = qt[base + QK_NOPE_DIM:base + QK_DIM]
        q_rope_swapped = qt[base + QK_DIM:base + Q_HEAD_ROWS]
        qt_ref[h, :QK_NOPE_DIM, :] = (q_nope * SCORE_SCALE).astype(BF16)
        qt_ref[h, QK_NOPE_DIM:, :] = ((q_rope * cct + q_rope_swapped * sst) * SCORE_SCALE).astype(BF16)

    k_nope = _dot(c_kv, w_kn_ref[...])
    for h in range(MLA_HEADS):
        k_ref[h, :, :QK_NOPE_DIM] = k_nope[:, h * QK_NOPE_DIM:(h + 1) * QK_NOPE_DIM].astype(BF16)
        k_ref[h, :, QK_NOPE_DIM:] = k_rot

    vt = _dot_nt(w_vt_ref[...], c_kv)
    for h in range(MLA_HEADS):
        vt_ref[h] = vt[h * V_HEAD_DIM:(h + 1) * V_HEAD_DIM].astype(BF16)


def _mla_proj(x, g_pre, w_in_ext, gq, gkv, w_uqt, w_kn, w_vt, cct, sst, cc, ss):
    b, s, d = x.shape
    tm = ROW_TILE
    h = MLA_HEADS
    row = lambda i, j: (i, j, 0)
    col = lambda i, j: (i, 0, j)
    return pl.pallas_call(
        _mla_proj_kernel,
        grid=(b, s // tm),
        in_specs=[pl.BlockSpec((None, tm, d), row),
                  _resident(g_pre.shape), _resident(w_in_ext.shape), _resident(gq.shape),
                  _resident(gkv.shape), _resident(w_uqt.shape), _resident(w_kn.shape),
                  _resident(w_vt.shape),
                  pl.BlockSpec((None, QK_ROPE_DIM, tm), col), pl.BlockSpec((None, QK_ROPE_DIM, tm), col),
                  pl.BlockSpec((None, tm, QK_ROPE_DIM), row), pl.BlockSpec((None, tm, QK_ROPE_DIM), row)],
        out_specs=[pl.BlockSpec((None, h, QK_DIM, tm), lambda i, j: (i, 0, 0, j)),
                   pl.BlockSpec((None, h, tm, QK_DIM), lambda i, j: (i, 0, j, 0)),
                   pl.BlockSpec((None, h, V_HEAD_DIM, tm), lambda i, j: (i, 0, 0, j))],
        out_shape=[jax.ShapeDtypeStruct((b, h, QK_DIM, s), BF16),
                   jax.ShapeDtypeStruct((b, h, s, QK_DIM), BF16),
                   jax.ShapeDtypeStruct((b, h, V_HEAD_DIM, s), BF16)],
        compiler_params=pltpu.CompilerParams(dimension_semantics=("parallel", "parallel"),
                                             vmem_limit_bytes=48 * MIB),
        name="mla_proj",
    )(x, g_pre, w_in_ext, gq, gkv, w_uqt, w_kn, w_vt, cct, sst, cc, ss)


def _flash_kernel(qt_ref, k_ref, vt_ref, o_ref, m_ref, l_ref, acc_ref, *, tq, tk):
    qi = pl.program_id(2)
    qt = qt_ref[...]
    m_ref[...] = jnp.full(m_ref.shape, -jnp.inf, F32)
    l_ref[...] = jnp.zeros(l_ref.shape, F32)
    acc_ref[...] = jnp.zeros(acc_ref.shape, F32)

    def block(j, masked):
        start = pl.multiple_of(j * tk, tk)
        s = _dot(k_ref[pl.ds(start, tk), :], qt)
        if masked:
            key_pos = start + lax.broadcasted_iota(jnp.int32, (tk, tq), 0)
            query_pos = qi * tq + lax.broadcasted_iota(jnp.int32, (tk, tq), 1)
            s = jnp.where(key_pos <= query_pos, s, -jnp.inf)
        m_prev = m_ref[...]
        m_new = jnp.maximum(m_prev, jnp.max(s, axis=0, keepdims=True))
        alpha = jnp.exp2(m_prev - m_new)
        p = jnp.exp2(s - m_new)
        l_ref[...] = alpha * l_ref[...] + jnp.sum(p, axis=0, keepdims=True)
        pv = _dot(vt_ref[:, pl.ds(start, tk)], p.astype(BF16))
        acc_ref[...] = alpha * acc_ref[...] + pv
        m_ref[...] = m_new

    n_full = (qi * tq) // tk
    lax.fori_loop(0, n_full, lambda j, c: (block(j, False), c)[1], 0)
    for d in range(tq // tk):
        block(n_full + d, True)

    o_ref[...] = jnp.transpose(acc_ref[...] / l_ref[...]).astype(o_ref.dtype)


def _flash_attention(qt, k, vt):
    b, h, _, s = qt.shape
    tq, tk = ATTN_Q_TILE, ATTN_K_TILE
    return pl.pallas_call(
        functools.partial(_flash_kernel, tq=tq, tk=tk),
        grid=(b, h, s // tq),
        in_specs=[pl.BlockSpec((None, None, QK_DIM, tq), lambda i, j, q: (i, j, 0, q)),
                  pl.BlockSpec((None, None, s, QK_DIM), lambda i, j, q: (i, j, 0, 0)),
                  pl.BlockSpec((None, None, V_HEAD_DIM, s), lambda i, j, q: (i, j, 0, 0))],
        out_specs=pl.BlockSpec((None, tq, V_HEAD_DIM), lambda i, j, q: (i, q, j)),
        out_shape=jax.ShapeDtypeStruct((b, s, h * V_HEAD_DIM), BF16),
        scratch_shapes=[pltpu.VMEM((1, tq), F32), pltpu.VMEM((1, tq), F32),
                        pltpu.VMEM((V_HEAD_DIM, tq), F32)],
        compiler_params=pltpu.CompilerParams(dimension_semantics=("parallel", "parallel", "arbitrary"),
                                             vmem_limit_bytes=48 * MIB),
        name="flash_attention",
    )(qt, k, vt)


def _lru_kernel(x_ref, g_ref, w_in_ref, conv_w_ref, conv_b_ref, w_gate_ref, b_a_ref, b_x_ref, lam_ref,
                o_ref, ext_ref, a_ref, b_ref, hs_ref, h_ref, *, tm):
    @pl.when(pl.program_id(1) == 0)
    def _():
        ext_ref[:SUBLANES, :] = jnp.zeros((SUBLANES, LRU_WIDTH), F32)
        h_ref[...] = jnp.zeros(h_ref.shape, F32)

    hn = _rms(x_ref[...], g_ref[...]).astype(BF16)
    proj = _dot(hn, w_in_ref[...])
    gate = jax.nn.gelu(proj[:, :LRU_WIDTH], approximate=True)

    ext_ref[SUBLANES:, :] = proj[:, LRU_WIDTH:]
    conv = ext_ref[pl.ds(SUBLANES - CONV_WIDTH + 1, tm), :] * conv_w_ref[0:1, :]
    for k in range(1, CONV_WIDTH):
        conv = conv + ext_ref[pl.ds(SUBLANES - CONV_WIDTH + 1 + k, tm), :] * conv_w_ref[k:k + 1, :]
    xc = conv_b_ref[...] + conv
    ext_ref[:SUBLANES, :] = ext_ref[tm:, :]

    xc_bf = xc.astype(BF16)
    neg_c_softplus = -LRU_C * jax.nn.softplus(-lam_ref[...])
    for n in range(LRU_BLOCKS):
        lanes = slice(n * LRU_BLOCK_SIZE, (n + 1) * LRU_BLOCK_SIZE)
        gates = _dot(xc_bf[:, lanes], w_gate_ref[n])
        r = jax.nn.sigmoid(gates[:, :LRU_BLOCK_SIZE] + b_a_ref[:, lanes])
        i = jax.nn.sigmoid(gates[:, LRU_BLOCK_SIZE:] + b_x_ref[:, lanes])
        log_a = neg_c_softplus[:, lanes] * r
        a = jnp.exp(log_a)
        a_ref[:, lanes] = a
        one_minus_a2 = jnp.tanh(-log_a) * (a * a + 1.0)
        b_ref[:, lanes] = jnp.sqrt(one_minus_a2) * (i * xc[:, lanes])

    def scan_row(t, h):
        h = a_ref[pl.ds(t, 1), :] * h + b_ref[pl.ds(t, 1), :]
        hs_ref[pl.ds(t, 1), :] = h
        return h

    h_ref[...] = lax.fori_loop(0, tm, scan_row, h_ref[...], unroll=8)
    o_ref[...] = (gate * hs_ref[...]).astype(o_ref.dtype)


def _lru_mixer(x, g_pre, w_in, conv_w, conv_b, w_gate, b_a, b_x, lam):
    b, s, d = x.shape
    tm = ROW_TILE
    row = lambda i, j: (i, j, 0)
    return pl.pallas_call(
        functools.partial(_lru_kernel, tm=tm),
        grid=(b, s // tm),
        in_specs=[pl.BlockSpec((None, tm, d), row),
                  _resident(g_pre.shape), _resident(w_in.shape), _resident(conv_w.shape),
                  _resident(conv_b.shape), _resident(w_gate.shape), _resident(b_a.shape),
                  _resident(b_x.shape), _resident(lam.shape)],
        out_specs=pl.BlockSpec((None, tm, LRU_WIDTH), row),
        out_shape=jax.ShapeDtypeStruct((b, s, LRU_WIDTH), BF16),
        scratch_shapes=[pltpu.VMEM((tm + SUBLANES, LRU_WIDTH), F32),
                        pltpu.VMEM((tm, LRU_WIDTH), F32), pltpu.VMEM((tm, LRU_WIDTH), F32),
                        pltpu.VMEM((tm, LRU_WIDTH), F32), pltpu.VMEM((1, LRU_WIDTH), F32)],
        compiler_params=pltpu.CompilerParams(dimension_semantics=("arbitrary", "arbitrary"),
                                             vmem_limit_bytes=48 * MIB),
        name="lru_mixer",
    )(x, g_pre, w_in, conv_w, conv_b, w_gate, b_a, b_x, lam)


def _post_ffn_kernel(x_ref, t_ref, w_out_ref, g_post_ref, g_pre_ref, w_up_ref, w_down_ref, g_ffn_ref,
                     o_ref):
    y = _dot(t_ref[...], w_out_ref[...])
    x1 = x_ref[...] + _rms(y, g_post_ref[...])
    hn = _rms(x1, g_pre_ref[...]).astype(BF16)
    acc = None
    for c in range(FFN_HIDDEN // FFN_CHUNK):
        cols = slice(c * FFN_CHUNK, (c + 1) * FFN_CHUNK)
        u = jnp.maximum(_dot(hn, w_up_ref[:, cols]), 0.0)
        part = _dot((u * u).astype(BF16), w_down_ref[cols, :])
        acc = part if acc is None else acc + part
    o_ref[...] = x1 + _rms(acc, g_ffn_ref[...])


def _post_ffn(x, t, w_out, g_post, g_pre, w_up, w_down, g_ffn):
    n, d = x.shape
    tm = ROW_TILE
    row = lambda i: (i, 0)
    return pl.pallas_call(
        _post_ffn_kernel,
        grid=(n // tm,),
        in_specs=[pl.BlockSpec((tm, d), row), pl.BlockSpec((tm, t.shape[1]), row),
                  _resident(w_out.shape), _resident(g_post.shape), _resident(g_pre.shape),
                  _resident(w_up.shape), _resident(w_down.shape), _resident(g_ffn.shape)],
        out_specs=pl.BlockSpec((tm, d), row),
        out_shape=jax.ShapeDtypeStruct((n, d), F32),
        compiler_params=pltpu.CompilerParams(dimension_semantics=("parallel",),
                                             vmem_limit_bytes=56 * MIB),
        name="post_ffn",
    )(x, t, w_out, g_post, g_pre, w_up, w_down, g_ffn)


def _swap_halves(w):
    half = w.shape[-1] // 2
    return jnp.concatenate([w[..., half:], w[..., :half]], axis=-1)


def _mla_weights(w_in, w_uq, w_ukv):
    kv_end = Q_LORA_RANK + KV_LORA_RANK
    w_in_ext = jnp.concatenate([w_in, _swap_halves(w_in[:, kv_end:])], axis=1).astype(BF16)
    uq = w_uq.reshape(Q_LORA_RANK, MLA_HEADS, QK_DIM)
    uq = jnp.concatenate([uq, _swap_halves(uq[..., QK_NOPE_DIM:])], axis=-1)
    w_uqt = uq.reshape(Q_LORA_RANK, MLA_HEADS * Q_HEAD_ROWS).T.astype(BF16)
    ukv = w_ukv.reshape(KV_LORA_RANK, MLA_HEADS, QK_NOPE_DIM + V_HEAD_DIM)
    w_kn = ukv[..., :QK_NOPE_DIM].reshape(KV_LORA_RANK, MLA_HEADS * QK_NOPE_DIM).astype(BF16)
    w_vt = ukv[..., QK_NOPE_DIM:].reshape(KV_LORA_RANK, MLA_HEADS * V_HEAD_DIM).T.astype(BF16)
    return w_in_ext, w_uqt, w_kn, w_vt


def kernel(x, positions, mix_pre_norm, mix_post_norm, ffn_pre_norm, ffn_post_norm, mla_w_in, mla_q_norm, mla_kv_norm, mla_w_uq, mla_w_ukv, mla_w_o, lru_w_in, lru_conv_w, lru_conv_b, lru_w_a, lru_b_a, lru_w_x, lru_b_x, lru_lambda, lru_w_out, ffn_w_up, ffn_w_down):
    b, s, d = x.shape
    depth = mix_pre_norm.shape[0]
    vec = lambda v: v.reshape(1, -1)

    inv_freq = ROPE_THETA ** (-jnp.arange(0, QK_ROPE_DIM, 2, dtype=F32) / QK_ROPE_DIM)
    cct, sst = _rope_tables(positions, inv_freq)
    cc, ss = jnp.swapaxes(cct, 1, 2), jnp.swapaxes(sst, 1, 2)

    for layer in range(depth):
        j = layer // 2
        g_pre = vec(mix_pre_norm[layer])
        if layer % 2 == 0:
            w_in_ext, w_uqt, w_kn, w_vt = _mla_weights(mla_w_in[j], mla_w_uq[j], mla_w_ukv[j])
            qt, k, vt = _mla_proj(x, g_pre, w_in_ext, vec(mla_q_norm[j]), vec(mla_kv_norm[j]),
                                  w_uqt, w_kn, w_vt, cct, sst, cc, ss)
            t = _flash_attention(qt, k, vt)
            w_out = mla_w_o[j]
        else:
            w_gate = jnp.concatenate([lru_w_a[j], lru_w_x[j]], axis=-1).astype(BF16)
            t = _lru_mixer(x, g_pre, lru_w_in[j].astype(BF16), lru_conv_w[j], vec(lru_conv_b[j]), w_gate,
                           vec(lru_b_a[j]), vec(lru_b_x[j]), vec(lru_lambda[j]))
            w_out = lru_w_out[j]
        x = _post_ffn(x.reshape(b * s, d), t.reshape(b * s, -1), w_out.astype(BF16),
                      vec(mix_post_norm[layer]), vec(ffn_pre_norm[layer]),
                      ffn_w_up[layer].astype(BF16), ffn_w_down[layer].astype(BF16),
                      vec(ffn_post_norm[layer])).reshape(b, s, d)
    return x
```

```python
import functools
import math

import jax
import jax.numpy as jnp
from jax import lax
from jax.experimental import pallas as pl
from jax.experimental.pallas import tpu as pltpu

D_MODEL = 1024
MLA_HEADS = 8
QK_NOPE_DIM = 128
QK_ROPE_DIM = 64
QK_DIM = QK_NOPE_DIM + QK_ROPE_DIM
V_HEAD_DIM = 128
Q_LORA_RANK = 384
KV_LORA_RANK = 256
ROPE_THETA = 10000.0
LRU_WIDTH = D_MODEL
LRU_BLOCKS = 8
LRU_BLOCK_SIZE = LRU_WIDTH // LRU_BLOCKS
CONV_WIDTH = 4
LRU_C = 8.0
FFN_HIDDEN = 4 * D_MODEL
NORM_EPS = 1e-6

SCORE_SCALE = math.log2(math.e) / math.sqrt(QK_DIM)

SUBLANES = 8
Q_HEAD_ROWS = 256
MIB = 1024 * 1024

ROW_TILE = 512
ROPE_TILE = 2048
ATTN_Q_TILE = 1024
ATTN_K_TILE = 512
SOFTMAX_ROWS = 16
FFN_CHUNK = 1024

BF16 = jnp.bfloat16
F32 = jnp.float32

_NT = (((1,), (1,)), ((), ()))


def _dot(a, b):
    return jnp.dot(a, b, preferred_element_type=F32)


def _dot_nt(a, b):
    return lax.dot_general(a, b, _NT, preferred_element_type=F32)


def _rms(x, g):
    return x * lax.rsqrt(jnp.mean(x * x, axis=-1, keepdims=True) + NORM_EPS) * g


def _resident(shape):
    zeros = (0,) * len(shape)
    return pl.BlockSpec(shape, lambda *_: zeros, pipeline_mode=pl.Buffered(1))


def _rope_table_kernel(pos_ref, inv_ref, cc_ref, ss_ref):
    ang = inv_ref[...] * pos_ref[...].astype(F32)
    c = jnp.cos(ang)
    s = jnp.sin(ang)
    cc_ref[...] = jnp.concatenate([c, c], axis=0)
    ss_ref[...] = jnp.concatenate([-s, s], axis=0)


def _rope_tables(positions, inv_freq):
    b, s = positions.shape
    half = QK_ROPE_DIM // 2
    out = jax.ShapeDtypeStruct((b, QK_ROPE_DIM, s), F32)
    return pl.pallas_call(
        _rope_table_kernel,
        grid=(b, s // ROPE_TILE),
        in_specs=[pl.BlockSpec((None, 1, ROPE_TILE), lambda i, j: (i, 0, j)),
                  pl.BlockSpec((half, 1), lambda i, j: (0, 0))],
        out_specs=[pl.BlockSpec((None, QK_ROPE_DIM, ROPE_TILE), lambda i, j: (i, 0, j))] * 2,
        out_shape=[out, out],
        name="rope_tables",
    )(positions.reshape(b, 1, s), inv_freq.reshape(half, 1))


def _mla_proj_kernel(x_ref, g_ref, w_in_ref, gq_ref, gkv_ref, w_uqt_ref, w_kn_ref, w_vt_ref,
                     cct_ref, sst_ref, cc_ref, ss_ref, qt_ref, k_ref, vt_ref):
    hn = _rms(x_ref[...], g_ref[...]).astype(BF16)
    proj = _dot(hn, w_in_ref[...])
    c_q = _rms(proj[:, :Q_LORA_RANK], gq_ref[...]).astype(BF16)
    kv_end = Q_LORA_RANK + KV_LORA_RANK
    c_kv = _rms(proj[:, Q_LORA_RANK:kv_end], gkv_ref[...]).astype(BF16)
    k_rope = proj[:, kv_end:kv_end + QK_ROPE_DIM]
    k_rope_swapped = proj[:, kv_end + QK_ROPE_DIM:]
    k_rot = (k_rope * cc_ref[...] + k_rope_swapped * ss_ref[...]).astype(BF16)

    qt = _dot_nt(w_uqt_ref[...], c_q)
    cct = cct_ref[...]
    sst = sst_ref[...]
    for h in range(MLA_HEADS):
        base = h * Q_HEAD_ROWS
        q_nope = qt[base:base + QK_NOPE_DIM]
        q_rope = qt[base + QK_NOPE_DIM:base + QK_DIM]
        q_rope_swapped = qt[base + QK_DIM:base + Q_HEAD_ROWS]
        qt_ref[h, :QK_NOPE_DIM, :] = (q_nope * SCORE_SCALE).astype(BF16)
        qt_ref[h, QK_NOPE_DIM:, :] = ((q_rope * cct + q_rope_swapped * sst) * SCORE_SCALE).astype(BF16)

    k_nope = _dot(c_kv, w_kn_ref[...])
    for h in range(MLA_HEADS):
        k_ref[h, :, :QK_NOPE_DIM] = k_nope[:, h * QK_NOPE_DIM:(h + 1) * QK_NOPE_DIM].astype(BF16)
        k_ref[h, :, QK_NOPE_DIM:] = k_rot

    vt = _dot_nt(w_vt_ref[...], c_kv)
    for h in range(MLA_HEADS):
        vt_ref[h] = vt[h * V_HEAD_DIM:(h + 1) * V_HEAD_DIM].astype(BF16)


def _mla_proj(x, g_pre, w_in_ext, gq, gkv, w_uqt, w_kn, w_vt, cct, sst, cc, ss):
    b, s, d = x.shape
    tm = ROW_TILE
    h = MLA_HEADS
    row = lambda i, j: (i, j, 0)
    col = lambda i, j: (i, 0, j)
    return pl.pallas_call(
        _mla_proj_kernel,
        grid=(b, s // tm),
        in_specs=[pl.BlockSpec((None, tm, d), row),
                  _resident(g_pre.shape), _resident(w_in_ext.shape), _resident(gq.shape),
                  _resident(gkv.shape), _resident(w_uqt.shape), _resident(w_kn.shape),
                  _resident(w_vt.shape),
                  pl.BlockSpec((None, QK_ROPE_DIM, tm), col), pl.BlockSpec((None, QK_ROPE_DIM, tm), col),
                  pl.BlockSpec((None, tm, QK_ROPE_DIM), row), pl.BlockSpec((None, tm, QK_ROPE_DIM), row)],
        out_specs=[pl.BlockSpec((None, h, QK_DIM, tm), lambda i, j: (i, 0, 0, j)),
                   pl.BlockSpec((None, h, tm, QK_DIM), lambda i, j: (i, 0, j, 0)),
                   pl.BlockSpec((None, h, V_HEAD_DIM, tm), lambda i, j: (i, 0, 0, j))],
        out_shape=[jax.ShapeDtypeStruct((b, h, QK_DIM, s), BF16),
                   jax.ShapeDtypeStruct((b, h, s, QK_DIM), BF16),
                   jax.ShapeDtypeStruct((b, h, V_HEAD_DIM, s), BF16)],
        compiler_params=pltpu.CompilerParams(dimension_semantics=("parallel", "parallel"),
                                             vmem_limit_bytes=48 * MIB),
        name="mla_proj",
    )(x, g_pre, w_in_ext, gq, gkv, w_uqt, w_kn, w_vt, cct, sst, cc, ss)


def _flash_kernel(qt_ref, k_ref, vt_ref, o_ref, s_a, s_b, p_a, p_b, cm_a, cm_b, al_a, al_b,
                  m_ref, l_ref, acc_ref, *, tq, tk):
    assert tq == 2 * tk
    qi = pl.program_id(2)
    m_ref[...] = jnp.full(m_ref.shape, -jnp.inf, F32)
    l_ref[...] = jnp.zeros(l_ref.shape, F32)
    acc_ref[...] = jnp.zeros(acc_ref.shape, F32)

    def scores(j, s_ref, cm_ref, diag=None):
        start = pl.multiple_of(j * tk, tk)
        s = _dot(k_ref[pl.ds(start, tk), :], qt_ref[...])
        if diag is not None:
            key = diag * tk + lax.broadcasted_iota(jnp.int32, (tk, tq), 0)
            query = lax.broadcasted_iota(jnp.int32, (tk, tq), 1)
            s = jnp.where(key <= query, s, -jnp.inf)
        s_ref[...] = s
        cm_ref[...] = jnp.max(s, axis=0, keepdims=True)

    def softmax(s_ref, cm_ref, p_ref, al_ref):
        m_prev = m_ref[...]
        m_new = jnp.maximum(m_prev, cm_ref[...])
        alpha = jnp.exp2(m_prev - m_new)
        m_ref[...] = m_new
        al_ref[...] = alpha
        partial = jnp.zeros((SUBLANES, tq), F32)
        for r in range(0, tk, SOFTMAX_ROWS):
            p = jnp.exp2(s_ref[r:r + SOFTMAX_ROWS, :] - m_new)
            p_ref[r:r + SOFTMAX_ROWS, :] = p.astype(BF16)
            for g in range(0, SOFTMAX_ROWS, SUBLANES):
                partial = partial + p[g:g + SUBLANES, :]
        l_ref[...] = alpha * l_ref[...] + jnp.sum(partial, axis=0, keepdims=True)

    def values(j, p_ref, al_ref):
        start = pl.multiple_of(j * tk, tk)
        acc_ref[...] = al_ref[...] * acc_ref[...] + _dot(vt_ref[:, pl.ds(start, tk)], p_ref[...])

    @pl.when(qi == 0)
    def _():
        scores(0, s_a, cm_a, diag=0)
        scores(1, s_b, cm_b, diag=1)
        softmax(s_a, cm_a, p_a, al_a)
        values(0, p_a, al_a)
        softmax(s_b, cm_b, p_b, al_b)
        values(1, p_b, al_b)

    @pl.when(qi > 0)
    def _():
        p_b[...] = jnp.zeros(p_b.shape, BF16)
        al_b[...] = jnp.ones(al_b.shape, F32)
        scores(0, s_a, cm_a)

        def pair(jj, carry):
            j = 2 * jj
            scores(j + 1, s_b, cm_b)
            softmax(s_a, cm_a, p_a, al_a)
            values(jnp.maximum(j - 1, 0), p_b, al_b)
            scores(j + 2, s_a, cm_a)
            softmax(s_b, cm_b, p_b, al_b)
            values(j, p_a, al_a)
            return carry

        lax.fori_loop(0, qi - 1, pair, 0)
        d0 = 2 * qi
        scores(d0 - 1, s_b, cm_b)
        softmax(s_a, cm_a, p_a, al_a)
        values(jnp.maximum(d0 - 3, 0), p_b, al_b)
        scores(d0, s_a, cm_a, diag=0)
        softmax(s_b, cm_b, p_b, al_b)
        values(d0 - 2, p_a, al_a)
        scores(d0 + 1, s_b, cm_b, diag=1)
        softmax(s_a, cm_a, p_a, al_a)
        values(d0 - 1, p_b, al_b)
        softmax(s_b, cm_b, p_b, al_b)
        values(d0, p_a, al_a)
        values(d0 + 1, p_b, al_b)

    o_ref[...] = jnp.transpose(acc_ref[...] / l_ref[...]).astype(o_ref.dtype)


def _flash_attention(qt, k, vt):
    b, h, _, s = qt.shape
    tq, tk = ATTN_Q_TILE, ATTN_K_TILE
    score_buf = pltpu.VMEM((tk, tq), F32)
    prob_buf = pltpu.VMEM((tk, tq), BF16)
    stat = pltpu.VMEM((1, tq), F32)
    return pl.pallas_call(
        functools.partial(_flash_kernel, tq=tq, tk=tk),
        grid=(b, h, s // tq),
        in_specs=[pl.BlockSpec((None, None, QK_DIM, tq), lambda i, j, q: (i, j, 0, q)),
                  pl.BlockSpec((None, None, s, QK_DIM), lambda i, j, q: (i, j, 0, 0)),
                  pl.BlockSpec((None, None, V_HEAD_DIM, s), lambda i, j, q: (i, j, 0, 0))],
        out_specs=pl.BlockSpec((None, tq, V_HEAD_DIM), lambda i, j, q: (i, q, j)),
        out_shape=jax.ShapeDtypeStruct((b, s, h * V_HEAD_DIM), BF16),
        scratch_shapes=[score_buf, score_buf, prob_buf, prob_buf, stat, stat, stat, stat, stat, stat,
                        pltpu.VMEM((V_HEAD_DIM, tq), F32)],
        compiler_params=pltpu.CompilerParams(dimension_semantics=("parallel", "parallel", "arbitrary"),
                                             vmem_limit_bytes=48 * MIB),
        name="flash_attention",
    )(qt, k, vt)


def _lru_kernel(x_ref, g_ref, w_in_ref, conv_w_ref, conv_b_ref, w_gate_ref, b_a_ref, b_x_ref, lam_ref,
                o_ref, ext_ref, a_ref, b_ref, hs_ref, h_ref, *, tm):
    @pl.when(pl.program_id(1) == 0)
    def _():
        ext_ref[:SUBLANES, :] = jnp.zeros((SUBLANES, LRU_WIDTH), F32)
        h_ref[...] = jnp.zeros(h_ref.shape, F32)

    hn = _rms(x_ref[...], g_ref[...]).astype(BF16)
    proj = _dot(hn, w_in_ref[...])
    gate = jax.nn.gelu(proj[:, :LRU_WIDTH], approximate=True)

    ext_ref[SUBLANES:, :] = proj[:, LRU_WIDTH:]
    conv = ext_ref[pl.ds(SUBLANES - CONV_WIDTH + 1, tm), :] * conv_w_ref[0:1, :]
    for k in range(1, CONV_WIDTH):
        conv = conv + ext_ref[pl.ds(SUBLANES - CONV_WIDTH + 1 + k, tm), :] * conv_w_ref[k:k + 1, :]
    xc = conv_b_ref[...] + conv
    ext_ref[:SUBLANES, :] = ext_ref[tm:, :]

    xc_bf = xc.astype(BF16)
    neg_c_softplus = -LRU_C * jax.nn.softplus(-lam_ref[...])
    for n in range(LRU_BLOCKS):
        lanes = slice(n * LRU_BLOCK_SIZE, (n + 1) * LRU_BLOCK_SIZE)
        gates = _dot(xc_bf[:, lanes], w_gate_ref[n])
        r = jax.nn.sigmoid(gates[:, :LRU_BLOCK_SIZE] + b_a_ref[:, lanes])
        i = jax.nn.sigmoid(gates[:, LRU_BLOCK_SIZE:] + b_x_ref[:, lanes])
        log_a = neg_c_softplus[:, lanes] * r
        a = jnp.exp(log_a)
        a_ref[:, lanes] = a
        one_minus_a2 = jnp.tanh(-log_a) * (a * a + 1.0)
        b_ref[:, lanes] = jnp.sqrt(one_minus_a2) * (i * xc[:, lanes])

    def scan_row(t, h):
        h = a_ref[pl.ds(t, 1), :] * h + b_ref[pl.ds(t, 1), :]
        hs_ref[pl.ds(t, 1), :] = h
        return h

    h_ref[...] = lax.fori_loop(0, tm, scan_row, h_ref[...], unroll=8)
    o_ref[...] = (gate * hs_ref[...]).astype(o_ref.dtype)


def _lru_mixer(x, g_pre, w_in, conv_w, conv_b, w_gate, b_a, b_x, lam):
    b, s, d = x.shape
    tm = ROW_TILE
    row = lambda i, j: (i, j, 0)
    return pl.pallas_call(
        functools.partial(_lru_kernel, tm=tm),
        grid=(b, s // tm),
        in_specs=[pl.BlockSpec((None, tm, d), row),
                  _resident(g_pre.shape), _resident(w_in.shape), _resident(conv_w.shape),
                  _resident(conv_b.shape), _resident(w_gate.shape), _resident(b_a.shape),
                  _resident(b_x.shape), _resident(lam.shape)],
        out_specs=pl.BlockSpec((None, tm, LRU_WIDTH), row),
        out_shape=jax.ShapeDtypeStruct((b, s, LRU_WIDTH), BF16),
        scratch_shapes=[pltpu.VMEM((tm + SUBLANES, LRU_WIDTH), F32),
                        pltpu.VMEM((tm, LRU_WIDTH), F32), pltpu.VMEM((tm, LRU_WIDTH), F32),
                        pltpu.VMEM((tm, LRU_WIDTH), F32), pltpu.VMEM((1, LRU_WIDTH), F32)],
        compiler_params=pltpu.CompilerParams(dimension_semantics=("arbitrary", "arbitrary"),
                                             vmem_limit_bytes=48 * MIB),
        name="lru_mixer",
    )(x, g_pre, w_in, conv_w, conv_b, w_gate, b_a, b_x, lam)


def _post_ffn_kernel(x_ref, t_ref, w_out_ref, g_post_ref, g_pre_ref, w_up_ref, w_down_ref, g_ffn_ref,
                     o_ref):
    y = _dot(t_ref[...], w_out_ref[...])
    x1 = x_ref[...] + _rms(y, g_post_ref[...])
    hn = _rms(x1, g_pre_ref[...]).astype(BF16)
    acc = None
    for c in range(FFN_HIDDEN // FFN_CHUNK):
        cols = slice(c * FFN_CHUNK, (c + 1) * FFN_CHUNK)
        u = jnp.maximum(_dot(hn, w_up_ref[:, cols]), 0.0)
        part = _dot((u * u).astype(BF16), w_down_ref[cols, :])
        acc = part if acc is None else acc + part
    o_ref[...] = x1 + _rms(acc, g_ffn_ref[...])


def _post_ffn(x, t, w_out, g_post, g_pre, w_up, w_down, g_ffn):
    n, d = x.shape
    tm = ROW_TILE
    row = lambda i: (i, 0)
    return pl.pallas_call(
        _post_ffn_kernel,
        grid=(n // tm,),
        in_specs=[pl.BlockSpec((tm, d), row), pl.BlockSpec((tm, t.shape[1]), row),
                  _resident(w_out.shape), _resident(g_post.shape), _resident(g_pre.shape),
                  _resident(w_up.shape), _resident(w_down.shape), _resident(g_ffn.shape)],
        out_specs=pl.BlockSpec((tm, d), row),
        out_shape=jax.ShapeDtypeStruct((n, d), F32),
        compiler_params=pltpu.CompilerParams(dimension_semantics=("parallel",),
                                             vmem_limit_bytes=56 * MIB),
        name="post_ffn",
    )(x, t, w_out, g_post, g_pre, w_up, w_down, g_ffn)


def _swap_halves(w):
    half = w.shape[-1] // 2
    return jnp.concatenate([w[..., half:], w[..., :half]], axis=-1)


def _mla_weights(w_in, w_uq, w_ukv):
    kv_end = Q_LORA_RANK + KV_LORA_RANK
    w_in_ext = jnp.concatenate([w_in, _swap_halves(w_in[:, kv_end:])], axis=1).astype(BF16)
    uq = w_uq.reshape(Q_LORA_RANK, MLA_HEADS, QK_DIM)
    uq = jnp.concatenate([uq, _swap_halves(uq[..., QK_NOPE_DIM:])], axis=-1)
    w_uqt = uq.reshape(Q_LORA_RANK, MLA_HEADS * Q_HEAD_ROWS).T.astype(BF16)
    ukv = w_ukv.reshape(KV_LORA_RANK, MLA_HEADS, QK_NOPE_DIM + V_HEAD_DIM)
    w_kn = ukv[..., :QK_NOPE_DIM].reshape(KV_LORA_RANK, MLA_HEADS * QK_NOPE_DIM).astype(BF16)
    w_vt = ukv[..., QK_NOPE_DIM:].reshape(KV_LORA_RANK, MLA_HEADS * V_HEAD_DIM).T.astype(BF16)
    return w_in_ext, w_uqt, w_kn, w_vt


def kernel(x, positions, mix_pre_norm, mix_post_norm, ffn_pre_norm, ffn_post_norm, mla_w_in, mla_q_norm, mla_kv_norm, mla_w_uq, mla_w_ukv, mla_w_o, lru_w_in, lru_conv_w, lru_conv_b, lru_w_a, lru_b_a, lru_w_x, lru_b_x, lru_lambda, lru_w_out, ffn_w_up, ffn_w_down):
    b, s, d = x.shape
    depth = mix_pre_norm.shape[0]
    vec = lambda v: v.reshape(1, -1)

    inv_freq = ROPE_THETA ** (-jnp.arange(0, QK_ROPE_DIM, 2, dtype=F32) / QK_ROPE_DIM)
    cct, sst = _rope_tables(positions, inv_freq)
    cc, ss = jnp.swapaxes(cct, 1, 2), jnp.swapaxes(sst, 1, 2)

    for layer in range(depth):
        j = layer // 2
        g_pre = vec(mix_pre_norm[layer])
        if layer % 2 == 0:
            w_in_ext, w_uqt, w_kn, w_vt = _mla_weights(mla_w_in[j], mla_w_uq[j], mla_w_ukv[j])
            qt, k, vt = _mla_proj(x, g_pre, w_in_ext, vec(mla_q_norm[j]), vec(mla_kv_norm[j]),
                                  w_uqt, w_kn, w_vt, cct, sst, cc, ss)
            t = _flash_attention(qt, k, vt)
            w_out = mla_w_o[j]
        else:
            w_gate = jnp.concatenate([lru_w_a[j], lru_w_x[j]], axis=-1).astype(BF16)
            t = _lru_mixer(x, g_pre, lru_w_in[j].astype(BF16), lru_conv_w[j], vec(lru_conv_b[j]), w_gate,
                           vec(lru_b_a[j]), vec(lru_b_x[j]), vec(lru_lambda[j]))
            w_out = lru_w_out[j]
        x = _post_ffn(x.reshape(b * s, d), t.reshape(b * s, -1), w_out.astype(BF16),
                      vec(mix_post_norm[layer]), vec(ffn_pre_norm[layer]),
                      ffn_w_up[layer].astype(BF16), ffn_w_down[layer].astype(BF16),
                      vec(ffn_post_norm[layer])).reshape(b, s, d)
    return x
```

```python
import functools
import math

import jax
import jax.numpy as jnp
from jax import lax
from jax.experimental import pallas as pl
from jax.experimental.pallas import tpu as pltpu

D_MODEL = 1024
MLA_HEADS = 8
QK_NOPE_DIM = 128
QK_ROPE_DIM = 64
QK_DIM = QK_NOPE_DIM + QK_ROPE_DIM
V_HEAD_DIM = 128
Q_LORA_RANK = 384
KV_LORA_RANK = 256
ROPE_THETA = 10000.0
LRU_WIDTH = D_MODEL
LRU_BLOCKS = 8
LRU_BLOCK_SIZE = LRU_WIDTH // LRU_BLOCKS
CONV_WIDTH = 4
LRU_C = 8.0
FFN_HIDDEN = 4 * D_MODEL
NORM_EPS = 1e-6

SCORE_SCALE = math.log2(math.e) / math.sqrt(QK_DIM)

SUBLANES = 8
Q_HEAD_ROWS = 256
MIB = 1024 * 1024

ROW_TILE = 512
ROPE_TILE = 2048
ATTN_Q_TILE = 1024
ATTN_K_TILE = 512
ATTN_COLS = 256
SOFTMAX_ROWS = 16
DENOM_ROWS = 16
FFN_CHUNK = 1024

BF16 = jnp.bfloat16
F32 = jnp.float32

_NT = (((1,), (1,)), ((), ()))


def _dot(a, b):
    return jnp.dot(a, b, preferred_element_type=F32)


def _dot_nt(a, b):
    return lax.dot_general(a, b, _NT, preferred_element_type=F32)


def _rms(x, g):
    return x * lax.rsqrt(jnp.mean(x * x, axis=-1, keepdims=True) + NORM_EPS) * g


def _resident(shape):
    zeros = (0,) * len(shape)
    return pl.BlockSpec(shape, lambda *_: zeros, pipeline_mode=pl.Buffered(1))


def _rope_table_kernel(pos_ref, inv_ref, cc_ref, ss_ref):
    ang = inv_ref[...] * pos_ref[...].astype(F32)
    c = jnp.cos(ang)
    s = jnp.sin(ang)
    cc_ref[...] = jnp.concatenate([c, c], axis=0)
    ss_ref[...] = jnp.concatenate([-s, s], axis=0)


def _rope_tables(positions, inv_freq):
    b, s = positions.shape
    half = QK_ROPE_DIM // 2
    out = jax.ShapeDtypeStruct((b, QK_ROPE_DIM, s), F32)
    return pl.pallas_call(
        _rope_table_kernel,
        grid=(b, s // ROPE_TILE),
        in_specs=[pl.BlockSpec((None, 1, ROPE_TILE), lambda i, j: (i, 0, j)),
                  pl.BlockSpec((half, 1), lambda i, j: (0, 0))],
        out_specs=[pl.BlockSpec((None, QK_ROPE_DIM, ROPE_TILE), lambda i, j: (i, 0, j))] * 2,
        out_shape=[out, out],
        name="rope_tables",
    )(positions.reshape(b, 1, s), inv_freq.reshape(half, 1))


def _mla_proj_kernel(x_ref, g_ref, w_in_ref, gq_ref, gkv_ref, w_uqt_ref, w_kn_ref, w_vt_ref,
                     cct_ref, sst_ref, cc_ref, ss_ref, qt_ref, k_ref, vt_ref):
    hn = _rms(x_ref[...], g_ref[...]).astype(BF16)
    proj = _dot(hn, w_in_ref[...])
    c_q = _rms(proj[:, :Q_LORA_RANK], gq_ref[...]).astype(BF16)
    kv_end = Q_LORA_RANK + KV_LORA_RANK
    c_kv = _rms(proj[:, Q_LORA_RANK:kv_end], gkv_ref[...]).astype(BF16)
    k_rope = proj[:, kv_end:kv_end + QK_ROPE_DIM]
    k_rope_swapped = proj[:, kv_end + QK_ROPE_DIM:]
    k_rot = (k_rope * cc_ref[...] + k_rope_swapped * ss_ref[...]).astype(BF16)

    qt = _dot_nt(w_uqt_ref[...], c_q)
    cct = cct_ref[...]
    sst = sst_ref[...]
    for h in range(MLA_HEADS):
        base = h * Q_HEAD_ROWS
        q_nope = qt[base:base + QK_NOPE_DIM]
        q_rope = qt[base + QK_NOPE_DIM:base + QK_DIM]
        q_rope_swapped = qt[base + QK_DIM:base + Q_HEAD_ROWS]
        qt_ref[h, :QK_NOPE_DIM, :] = (q_nope * SCORE_SCALE).astype(BF16)
        qt_ref[h, QK_NOPE_DIM:, :] = ((q_rope * cct + q_rope_swapped * sst) * SCORE_SCALE).astype(BF16)

    k_nope = _dot(c_kv, w_kn_ref[...])
    for h in range(MLA_HEADS):
        k_ref[h, :, :QK_NOPE_DIM] = k_nope[:, h * QK_NOPE_DIM:(h + 1) * QK_NOPE_DIM].astype(BF16)
        k_ref[h, :, QK_NOPE_DIM:] = k_rot

    vt = _dot_nt(w_vt_ref[...], c_kv)
    for h in range(MLA_HEADS):
        vt_ref[h] = vt[h * V_HEAD_DIM:(h + 1) * V_HEAD_DIM].astype(BF16)


def _mla_proj(x, g_pre, w_in_ext, gq, gkv, w_uqt, w_kn, w_vt, cct, sst, cc, ss):
    b, s, d = x.shape
    tm = ROW_TILE
    h = MLA_HEADS
    row = lambda i, j: (i, j, 0)
    col = lambda i, j: (i, 0, j)
    return pl.pallas_call(
        _mla_proj_kernel,
        grid=(b, s // tm),
        in_specs=[pl.BlockSpec((None, tm, d), row),
                  _resident(g_pre.shape), _resident(w_in_ext.shape), _resident(gq.shape),
                  _resident(gkv.shape), _resident(w_uqt.shape), _resident(w_kn.shape),
                  _resident(w_vt.shape),
                  pl.BlockSpec((None, QK_ROPE_DIM, tm), col), pl.BlockSpec((None, QK_ROPE_DIM, tm), col),
                  pl.BlockSpec((None, tm, QK_ROPE_DIM), row), pl.BlockSpec((None, tm, QK_ROPE_DIM), row)],
        out_specs=[pl.BlockSpec((None, h, QK_DIM, tm), lambda i, j: (i, 0, 0, j)),
                   pl.BlockSpec((None, h, tm, QK_DIM), lambda i, j: (i, 0, j, 0)),
                   pl.BlockSpec((None, h, V_HEAD_DIM, tm), lambda i, j: (i, 0, 0, j))],
        out_shape=[jax.ShapeDtypeStruct((b, h, QK_DIM, s), BF16),
                   jax.ShapeDtypeStruct((b, h, s, QK_DIM), BF16),
                   jax.ShapeDtypeStruct((b, h, V_HEAD_DIM, s), BF16)],
        compiler_params=pltpu.CompilerParams(dimension_semantics=("parallel", "parallel"),
                                             vmem_limit_bytes=48 * MIB),
        name="mla_proj",
    )(x, g_pre, w_in_ext, gq, gkv, w_uqt, w_kn, w_vt, cct, sst, cc, ss)


def _flash_kernel(qt_ref, k_ref, vt_ref, o_ref, s_a, s_b, p_a, p_b, cm_a, cm_b, al_a, al_b,
                  m_ref, acc_ref, *, tq, tk):
    assert tq == 2 * tk
    qi = pl.program_id(2)
    m_ref[...] = jnp.full(m_ref.shape, -jnp.inf, F32)
    acc_ref[...] = jnp.zeros(acc_ref.shape, F32)

    ones_rows = jnp.where(lax.broadcasted_iota(jnp.int32, (DENOM_ROWS, tk), 0) == 0, 1.0, 0.0).astype(BF16)

    def scores(c, j, s_ref, cm_ref, diag=None):
        cols = slice(c * ATTN_COLS, (c + 1) * ATTN_COLS)
        first_key, first_query = (diag or 0) * tk, c * ATTN_COLS
        if diag is not None and first_key > first_query + ATTN_COLS - 1:
            s_ref[:, cols] = jnp.full((tk, ATTN_COLS), -jnp.inf, F32)
            cm_ref[:, cols] = jnp.full((1, ATTN_COLS), -jnp.inf, F32)
            return
        start = pl.multiple_of(j * tk, tk)
        s = _dot(k_ref[pl.ds(start, tk), :], qt_ref[:, cols])
        if diag is not None and first_key + tk - 1 > first_query:
            key = first_key + lax.broadcasted_iota(jnp.int32, (tk, ATTN_COLS), 0)
            query = first_query + lax.broadcasted_iota(jnp.int32, (tk, ATTN_COLS), 1)
            s = jnp.where(key <= query, s, -jnp.inf)
        s_ref[:, cols] = s
        cm_ref[:, cols] = jnp.max(s, axis=0, keepdims=True)

    def softmax(c, s_ref, cm_ref, p_ref, al_ref):
        cols = slice(c * ATTN_COLS, (c + 1) * ATTN_COLS)
        m_prev = m_ref[:, cols]
        m_new = jnp.maximum(m_prev, cm_ref[:, cols])
        m_ref[:, cols] = m_new
        al_ref[:, cols] = jnp.exp2(m_prev - m_new)
        for r in range(0, tk, SOFTMAX_ROWS):
            rows = slice(r, r + SOFTMAX_ROWS)
            p_ref[rows, cols] = jnp.exp2((s_ref[rows, cols] - m_new).astype(BF16))

    def values(c, j, p_ref, al_ref):
        cols = slice(c * ATTN_COLS, (c + 1) * ATTN_COLS)
        start = pl.multiple_of(j * tk, tk)
        vt_aug = jnp.concatenate([vt_ref[:, pl.ds(start, tk)], ones_rows], axis=0)
        acc_ref[:, cols] = al_ref[:, cols] * acc_ref[:, cols] + _dot(vt_aug, p_ref[:, cols])

    def phase(score_args=None, softmax_args=None, value_args=None):
        for c in range(tq // ATTN_COLS):
            if score_args is not None:
                scores(c, *score_args)
            if softmax_args is not None:
                softmax(c, *softmax_args)
            if value_args is not None:
                values(c, *value_args)

    set_a = (s_a, cm_a, p_a, al_a)
    set_b = (s_b, cm_b, p_b, al_b)

    @pl.when(qi == 0)
    def _():
        phase(score_args=(0, s_a, cm_a, 0))
        phase(score_args=(1, s_b, cm_b, 1), softmax_args=set_a)
        phase(softmax_args=set_b, value_args=(0, p_a, al_a))
        phase(value_args=(1, p_b, al_b))

    @pl.when(qi > 0)
    def _():
        p_b[...] = jnp.zeros(p_b.shape, BF16)
        al_b[...] = jnp.ones(al_b.shape, F32)
        phase(score_args=(0, s_a, cm_a))

        def pair(jj, carry):
            j = 2 * jj
            phase((j + 1, s_b, cm_b), set_a, (jnp.maximum(j - 1, 0), p_b, al_b))
            phase((j + 2, s_a, cm_a), set_b, (j, p_a, al_a))
            return carry

        lax.fori_loop(0, qi - 1, pair, 0)
        d0 = 2 * qi
        phase((d0 - 1, s_b, cm_b), set_a, (jnp.maximum(d0 - 3, 0), p_b, al_b))
        phase((d0, s_a, cm_a, 0), set_b, (d0 - 2, p_a, al_a))
        phase((d0 + 1, s_b, cm_b, 1), set_a, (d0 - 1, p_b, al_b))
        phase(None, set_b, (d0, p_a, al_a))
        phase(None, None, (d0 + 1, p_b, al_b))

    out_t = acc_ref[:V_HEAD_DIM, :] / acc_ref[V_HEAD_DIM:V_HEAD_DIM + 1, :]
    o_ref[...] = jnp.transpose(out_t).astype(o_ref.dtype)


def _flash_attention(qt, k, vt):
    b, h, _, s = qt.shape
    tq, tk = ATTN_Q_TILE, ATTN_K_TILE
    score_buf = pltpu.VMEM((tk, tq), F32)
    prob_buf = pltpu.VMEM((tk, tq), BF16)
    stat = pltpu.VMEM((1, tq), F32)
    return pl.pallas_call(
        functools.partial(_flash_kernel, tq=tq, tk=tk),
        grid=(b, h, s // tq),
        in_specs=[pl.BlockSpec((None, None, QK_DIM, tq), lambda i, j, q: (i, j, 0, q)),
                  pl.BlockSpec((None, None, s, QK_DIM), lambda i, j, q: (i, j, 0, 0)),
                  pl.BlockSpec((None, None, V_HEAD_DIM, s), lambda i, j, q: (i, j, 0, 0))],
        out_specs=pl.BlockSpec((None, tq, V_HEAD_DIM), lambda i, j, q: (i, q, j)),
        out_shape=jax.ShapeDtypeStruct((b, s, h * V_HEAD_DIM), BF16),
        scratch_shapes=[score_buf, score_buf, prob_buf, prob_buf, stat, stat, stat, stat, stat,
                        pltpu.VMEM((V_HEAD_DIM + DENOM_ROWS, tq), F32)],
        compiler_params=pltpu.CompilerParams(dimension_semantics=("parallel", "parallel", "arbitrary"),
                                             vmem_limit_bytes=48 * MIB),
        name="flash_attention",
    )(qt, k, vt)


def _lru_kernel(x_ref, g_ref, w_in_ref, conv_w_ref, conv_b_ref, w_gate_ref, b_a_ref, b_x_ref, lam_ref,
                o_ref, ext_ref, a_ref, b_ref, hs_ref, h_ref, *, tm):
    @pl.when(pl.program_id(1) == 0)
    def _():
        ext_ref[:SUBLANES, :] = jnp.zeros((SUBLANES, LRU_WIDTH), F32)
        h_ref[...] = jnp.zeros(h_ref.shape, F32)

    hn = _rms(x_ref[...], g_ref[...]).astype(BF16)
    proj = _dot(hn, w_in_ref[...])
    gate = jax.nn.gelu(proj[:, :LRU_WIDTH], approximate=True)

    ext_ref[SUBLANES:, :] = proj[:, LRU_WIDTH:]
    conv = ext_ref[pl.ds(SUBLANES - CONV_WIDTH + 1, tm), :] * conv_w_ref[0:1, :]
    for k in range(1, CONV_WIDTH):
        conv = conv + ext_ref[pl.ds(SUBLANES - CONV_WIDTH + 1 + k, tm), :] * conv_w_ref[k:k + 1, :]
    xc = conv_b_ref[...] + conv
    ext_ref[:SUBLANES, :] = ext_ref[tm:, :]

    xc_bf = xc.astype(BF16)
    neg_c_softplus = -LRU_C * jax.nn.softplus(-lam_ref[...])
    for n in range(LRU_BLOCKS):
        lanes = slice(n * LRU_BLOCK_SIZE, (n + 1) * LRU_BLOCK_SIZE)
        gates = _dot(xc_bf[:, lanes], w_gate_ref[n])
        r = jax.nn.sigmoid(gates[:, :LRU_BLOCK_SIZE] + b_a_ref[:, lanes])
        i = jax.nn.sigmoid(gates[:, LRU_BLOCK_SIZE:] + b_x_ref[:, lanes])
        log_a = neg_c_softplus[:, lanes] * r
        a = jnp.exp(log_a)
        a_ref[:, lanes] = a
        one_minus_a2 = jnp.tanh(-log_a) * (a * a + 1.0)
        b_ref[:, lanes] = jnp.sqrt(one_minus_a2) * (i * xc[:, lanes])

    def scan_row(t, h):
        h = a_ref[pl.ds(t, 1), :] * h + b_ref[pl.ds(t, 1), :]
        hs_ref[pl.ds(t, 1), :] = h
        return h

    h_ref[...] = lax.fori_loop(0, tm, scan_row, h_ref[...], unroll=8)
    o_ref[...] = (gate * hs_ref[...]).astype(o_ref.dtype)


def _lru_mixer(x, g_pre, w_in, conv_w, conv_b, w_gate, b_a, b_x, lam):
    b, s, d = x.shape
    tm = ROW_TILE
    row = lambda i, j: (i, j, 0)
    return pl.pallas_call(
        functools.partial(_lru_kernel, tm=tm),
        grid=(b, s // tm),
        in_specs=[pl.BlockSpec((None, tm, d), row),
                  _resident(g_pre.shape), _resident(w_in.shape), _resident(conv_w.shape),
                  _resident(conv_b.shape), _resident(w_gate.shape), _resident(b_a.shape),
                  _resident(b_x.shape), _resident(lam.shape)],
        out_specs=pl.BlockSpec((None, tm, LRU_WIDTH), row),
        out_shape=jax.ShapeDtypeStruct((b, s, LRU_WIDTH), BF16),
        scratch_shapes=[pltpu.VMEM((tm + SUBLANES, LRU_WIDTH), F32),
                        pltpu.VMEM((tm, LRU_WIDTH), F32), pltpu.VMEM((tm, LRU_WIDTH), F32),
                        pltpu.VMEM((tm, LRU_WIDTH), F32), pltpu.VMEM((1, LRU_WIDTH), F32)],
        compiler_params=pltpu.CompilerParams(dimension_semantics=("arbitrary", "arbitrary"),
                                             vmem_limit_bytes=48 * MIB),
        name="lru_mixer",
    )(x, g_pre, w_in, conv_w, conv_b, w_gate, b_a, b_x, lam)


def _post_ffn_kernel(x_ref, t_ref, w_out_ref, g_post_ref, g_pre_ref, w_up_ref, w_down_ref, g_ffn_ref,
                     o_ref):
    y = _dot(t_ref[...], w_out_ref[...])
    x1 = x_ref[...] + _rms(y, g_post_ref[...])
    hn = _rms(x1, g_pre_ref[...]).astype(BF16)
    acc = None
    for c in range(FFN_HIDDEN // FFN_CHUNK):
        cols = slice(c * FFN_CHUNK, (c + 1) * FFN_CHUNK)
        u = jnp.maximum(_dot(hn, w_up_ref[:, cols]), 0.0)
        part = _dot((u * u).astype(BF16), w_down_ref[cols, :])
        acc = part if acc is None else acc + part
    o_ref[...] = x1 + _rms(acc, g_ffn_ref[...])


def _post_ffn(x, t, w_out, g_post, g_pre, w_up, w_down, g_ffn):
    n, d = x.shape
    tm = ROW_TILE
    row = lambda i: (i, 0)
    return pl.pallas_call(
        _post_ffn_kernel,
        grid=(n // tm,),
        in_specs=[pl.BlockSpec((tm, d), row), pl.BlockSpec((tm, t.shape[1]), row),
                  _resident(w_out.shape), _resident(g_post.shape), _resident(g_pre.shape),
                  _resident(w_up.shape), _resident(w_down.shape), _resident(g_ffn.shape)],
        out_specs=pl.BlockSpec((tm, d), row),
        out_shape=jax.ShapeDtypeStruct((n, d), F32),
        compiler_params=pltpu.CompilerParams(dimension_semantics=("parallel",),
                                             vmem_limit_bytes=56 * MIB),
        name="post_ffn",
    )(x, t, w_out, g_post, g_pre, w_up, w_down, g_ffn)


def _swap_halves(w):
    half = w.shape[-1] // 2
    return jnp.concatenate([w[..., half:], w[..., :half]], axis=-1)


def _mla_weights(w_in, w_uq, w_ukv):
    kv_end = Q_LORA_RANK + KV_LORA_RANK
    w_in_ext = jnp.concatenate([w_in, _swap_halves(w_in[:, kv_end:])], axis=1).astype(BF16)
    uq = w_uq.reshape(Q_LORA_RANK, MLA_HEADS, QK_DIM)
    uq = jnp.concatenate([uq, _swap_halves(uq[..., QK_NOPE_DIM:])], axis=-1)
    w_uqt = uq.reshape(Q_LORA_RANK, MLA_HEADS * Q_HEAD_ROWS).T.astype(BF16)
    ukv = w_ukv.reshape(KV_LORA_RANK, MLA_HEADS, QK_NOPE_DIM + V_HEAD_DIM)
    w_kn = ukv[..., :QK_NOPE_DIM].reshape(KV_LORA_RANK, MLA_HEADS * QK_NOPE_DIM).astype(BF16)
    w_vt = ukv[..., QK_NOPE_DIM:].reshape(KV_LORA_RANK, MLA_HEADS * V_HEAD_DIM).T.astype(BF16)
    return w_in_ext, w_uqt, w_kn, w_vt


def kernel(x, positions, mix_pre_norm, mix_post_norm, ffn_pre_norm, ffn_post_norm, mla_w_in, mla_q_norm, mla_kv_norm, mla_w_uq, mla_w_ukv, mla_w_o, lru_w_in, lru_conv_w, lru_conv_b, lru_w_a, lru_b_a, lru_w_x, lru_b_x, lru_lambda, lru_w_out, ffn_w_up, ffn_w_down):
    b, s, d = x.shape
    depth = mix_pre_norm.shape[0]
    vec = lambda v: v.reshape(1, -1)

    inv_freq = ROPE_THETA ** (-jnp.arange(0, QK_ROPE_DIM, 2, dtype=F32) / QK_ROPE_DIM)
    cct, sst = _rope_tables(positions, inv_freq)
    cc, ss = jnp.swapaxes(cct, 1, 2), jnp.swapaxes(sst, 1, 2)

    for layer in range(depth):
        j = layer // 2
        g_pre = vec(mix_pre_norm[layer])
        if layer % 2 == 0:
            w_in_ext, w_uqt, w_kn, w_vt = _mla_weights(mla_w_in[j], mla_w_uq[j], mla_w_ukv[j])
            qt, k, vt = _mla_proj(x, g_pre, w_in_ext, vec(mla_q_norm[j]), vec(mla_kv_norm[j]),
                                  w_uqt, w_kn, w_vt, cct, sst, cc, ss)
            t = _flash_attention(qt, k, vt)
            w_out = mla_w_o[j]
        else:
            w_gate = jnp.concatenate([lru_w_a[j], lru_w_x[j]], axis=-1).astype(BF16)
            t = _lru_mixer(x, g_pre, lru_w_in[j].astype(BF16), lru_conv_w[j], vec(lru_conv_b[j]), w_gate,
                           vec(lru_b_a[j]), vec(lru_b_x[j]), vec(lru_lambda[j]))
            w_out = lru_w_out[j]
        x = _post_ffn(x.reshape(b * s, d), t.reshape(b * s, -1), w_out.astype(BF16),
                      vec(mix_post_norm[layer]), vec(ffn_pre_norm[layer]),
                      ffn_w_up[layer].astype(BF16), ffn_w_down[layer].astype(BF16),
                      vec(ffn_post_norm[layer])).reshape(b, s, d)
    return x
```

```python
import functools
import math

import jax
import jax.numpy as jnp
from jax import lax
from jax.experimental import pallas as pl
from jax.experimental.pallas import tpu as pltpu

D_MODEL = 1024
MLA_HEADS = 8
QK_NOPE_DIM = 128
QK_ROPE_DIM = 64
QK_DIM = QK_NOPE_DIM + QK_ROPE_DIM
V_HEAD_DIM = 128
Q_LORA_RANK = 384
KV_LORA_RANK = 256
ROPE_THETA = 10000.0
LRU_WIDTH = D_MODEL
LRU_BLOCKS = 8
LRU_BLOCK_SIZE = LRU_WIDTH // LRU_BLOCKS
CONV_WIDTH = 4
LRU_C = 8.0
FFN_HIDDEN = 4 * D_MODEL
NORM_EPS = 1e-6

SCORE_SCALE = math.log2(math.e) / math.sqrt(QK_DIM)

SUBLANES = 8
MIB = 1024 * 1024

ROW_TILE = 512
ROPE_TILE = 2048
ATTN_Q_TILE = 1024
ATTN_K_TILE = 512
ATTN_COLS = 256
SOFTMAX_ROWS = 128
DENOM_ROWS = 16
FFN_CHUNK = 1024

BF16 = jnp.bfloat16
F32 = jnp.float32

_NT = (((1,), (1,)), ((), ()))


def _dot(a, b):
    return jnp.dot(a, b, preferred_element_type=F32)


def _dot_nt(a, b):
    return lax.dot_general(a, b, _NT, preferred_element_type=F32)


def _rms(x, g):
    return x * lax.rsqrt(jnp.mean(x * x, axis=-1, keepdims=True) + NORM_EPS) * g


def _resident(shape):
    zeros = (0,) * len(shape)
    return pl.BlockSpec(shape, lambda *_: zeros, pipeline_mode=pl.Buffered(1))


def _rope_table_kernel(pos_ref, inv_ref, cc_ref, ss_ref):
    ang = inv_ref[...] * pos_ref[...].astype(F32)
    c = jnp.cos(ang)
    s = jnp.sin(ang)
    cc_ref[...] = jnp.concatenate([c, c], axis=0)
    ss_ref[...] = jnp.concatenate([-s, s], axis=0)


def _rope_tables(positions, inv_freq):
    b, s = positions.shape
    half = QK_ROPE_DIM // 2
    out = jax.ShapeDtypeStruct((b, QK_ROPE_DIM, s), F32)
    return pl.pallas_call(
        _rope_table_kernel,
        grid=(b, s // ROPE_TILE),
        in_specs=[pl.BlockSpec((None, 1, ROPE_TILE), lambda i, j: (i, 0, j)),
                  pl.BlockSpec((half, 1), lambda i, j: (0, 0))],
        out_specs=[pl.BlockSpec((None, QK_ROPE_DIM, ROPE_TILE), lambda i, j: (i, 0, j))] * 2,
        out_shape=[out, out],
        name="rope_tables",
    )(positions.reshape(b, 1, s), inv_freq.reshape(half, 1))


def _mla_proj_kernel(x_ref, g_ref, w_in_ref, gq_ref, gkv_ref, w_uqt_ref, w_kn_ref, w_vt_ref,
                     cct_ref, sst_ref, cc_ref, ss_ref, qt_ref, k_ref, vt_ref):
    hn = _rms(x_ref[...], g_ref[...]).astype(BF16)
    proj = _dot(hn, w_in_ref[...])
    c_q = _rms(proj[:, :Q_LORA_RANK], gq_ref[...]).astype(BF16)
    kv_end = Q_LORA_RANK + KV_LORA_RANK
    c_kv = _rms(proj[:, Q_LORA_RANK:kv_end], gkv_ref[...]).astype(BF16)
    k_rope = proj[:, kv_end:kv_end + QK_ROPE_DIM]
    k_rope_swapped = proj[:, kv_end + QK_ROPE_DIM:]
    k_rot = (k_rope * cc_ref[...] + k_rope_swapped * ss_ref[...]).astype(BF16)

    qt = _dot_nt(w_uqt_ref[...], c_q)
    cct = cct_ref[...]
    sst = sst_ref[...]
    half = QK_ROPE_DIM // 2
    for h in range(MLA_HEADS):
        base = h * QK_DIM
        q_nope = qt[base:base + QK_NOPE_DIM]
        q_rope = qt[base + QK_NOPE_DIM:base + QK_DIM]
        q_rope_swapped = jnp.concatenate([q_rope[half:], q_rope[:half]], axis=0)
        qt_ref[h, :QK_NOPE_DIM, :] = (q_nope * SCORE_SCALE).astype(BF16)
        qt_ref[h, QK_NOPE_DIM:, :] = ((q_rope * cct + q_rope_swapped * sst) * SCORE_SCALE).astype(BF16)

    k_nope = _dot(c_kv, w_kn_ref[...])
    for h in range(MLA_HEADS):
        k_ref[h, :, :QK_NOPE_DIM] = k_nope[:, h * QK_NOPE_DIM:(h + 1) * QK_NOPE_DIM].astype(BF16)
        k_ref[h, :, QK_NOPE_DIM:] = k_rot

    vt = _dot_nt(w_vt_ref[...], c_kv)
    ones_rows = jnp.where(lax.broadcasted_iota(jnp.int32, (DENOM_ROWS, vt.shape[1]), 0) == 0, 1.0, 0.0)
    for h in range(MLA_HEADS):
        vt_ref[h, 0, :V_HEAD_DIM, :] = vt[h * V_HEAD_DIM:(h + 1) * V_HEAD_DIM].astype(BF16)
        vt_ref[h, 0, V_HEAD_DIM:, :] = ones_rows.astype(BF16)


def _mla_proj(x, g_pre, w_in_ext, gq, gkv, w_uqt, w_kn, w_vt, cct, sst, cc, ss):
    b, s, d = x.shape
    tm = ROW_TILE
    h = MLA_HEADS
    row = lambda i, j: (i, j, 0)
    col = lambda i, j: (i, 0, j)
    return pl.pallas_call(
        _mla_proj_kernel,
        grid=(b, s // tm),
        in_specs=[pl.BlockSpec((None, tm, d), row),
                  _resident(g_pre.shape), _resident(w_in_ext.shape), _resident(gq.shape),
                  _resident(gkv.shape), _resident(w_uqt.shape), _resident(w_kn.shape),
                  _resident(w_vt.shape),
                  pl.BlockSpec((None, QK_ROPE_DIM, tm), col), pl.BlockSpec((None, QK_ROPE_DIM, tm), col),
                  pl.BlockSpec((None, tm, QK_ROPE_DIM), row), pl.BlockSpec((None, tm, QK_ROPE_DIM), row)],
        out_specs=[pl.BlockSpec((None, h, QK_DIM, tm), lambda i, j: (i, 0, 0, j)),
                   pl.BlockSpec((None, h, tm, QK_DIM), lambda i, j: (i, 0, j, 0)),
                   pl.BlockSpec((None, h, 1, V_HEAD_DIM + DENOM_ROWS, tm), lambda i, j: (i, 0, j, 0, 0))],
        out_shape=[jax.ShapeDtypeStruct((b, h, QK_DIM, s), BF16),
                   jax.ShapeDtypeStruct((b, h, s, QK_DIM), BF16),
                   jax.ShapeDtypeStruct((b, h, s // tm, V_HEAD_DIM + DENOM_ROWS, tm), BF16)],
        compiler_params=pltpu.CompilerParams(dimension_semantics=("parallel", "parallel"),
                                             vmem_limit_bytes=48 * MIB),
        name="mla_proj",
    )(x, g_pre, w_in_ext, gq, gkv, w_uqt, w_kn, w_vt, cct, sst, cc, ss)


def _flash_kernel(qt_ref, k_ref, vt_ref, o_ref, s_a, s_b, p_a, p_b, cm_a, cm_b, al_a, al_b,
                  m_ref, acc_ref, *, tq, tk):
    assert tq == 2 * tk
    qi = pl.program_id(2)
    m_ref[...] = jnp.full(m_ref.shape, -jnp.inf, F32)
    acc_ref[...] = jnp.zeros(acc_ref.shape, F32)

    def scores(c, j, s_ref, cm_ref, diag=None):
        cols = slice(c * ATTN_COLS, (c + 1) * ATTN_COLS)
        first_key, first_query = (diag or 0) * tk, c * ATTN_COLS
        if diag is not None and first_key > first_query + ATTN_COLS - 1:
            s_ref[c] = jnp.full((tk, ATTN_COLS), -jnp.inf, F32)
            cm_ref[:, cols] = jnp.full((1, ATTN_COLS), -jnp.inf, F32)
            return
        start = pl.multiple_of(j * tk, tk)
        s = _dot(k_ref[pl.ds(start, tk), :], qt_ref[:, cols])
        if diag is not None and first_key + tk - 1 > first_query:
            key = first_key + lax.broadcasted_iota(jnp.int32, (tk, ATTN_COLS), 0)
            query = first_query + lax.broadcasted_iota(jnp.int32, (tk, ATTN_COLS), 1)
            s = jnp.where(key <= query, s, -jnp.inf)
        s_ref[c] = s
        cm_ref[:, cols] = jnp.max(s, axis=0, keepdims=True)

    def softmax(c, s_ref, cm_ref, p_ref, al_ref):
        cols = slice(c * ATTN_COLS, (c + 1) * ATTN_COLS)
        m_prev = m_ref[:, cols]
        m_new = jnp.maximum(m_prev, cm_ref[:, cols])
        m_ref[:, cols] = m_new
        al_ref[:, cols] = jnp.exp2(m_prev - m_new)
        for r in range(0, tk, SOFTMAX_ROWS):
            rows = slice(r, r + SOFTMAX_ROWS)
            p_ref[c, rows, :] = jnp.exp2(s_ref[c, rows, :] - m_new).astype(BF16)

    def values(c, j, p_ref, al_ref):
        cols = slice(c * ATTN_COLS, (c + 1) * ATTN_COLS)
        acc_ref[c] = al_ref[:, cols] * acc_ref[c] + _dot(vt_ref[j], p_ref[c])

    def phase(score_args=None, softmax_args=None, value_args=None):
        for c in range(tq // ATTN_COLS):
            if score_args is not None:
                scores(c, *score_args)
            if softmax_args is not None:
                softmax(c, *softmax_args)
            if value_args is not None:
                values(c, *value_args)

    set_a = (s_a, cm_a, p_a, al_a)
    set_b = (s_b, cm_b, p_b, al_b)

    @pl.when(qi == 0)
    def _():
        phase(score_args=(0, s_a, cm_a, 0))
        phase(score_args=(1, s_b, cm_b, 1), softmax_args=set_a)
        phase(softmax_args=set_b, value_args=(0, p_a, al_a))
        phase(value_args=(1, p_b, al_b))

    @pl.when(qi > 0)
    def _():
        p_b[...] = jnp.zeros(p_b.shape, BF16)
        al_b[...] = jnp.ones(al_b.shape, F32)
        phase(score_args=(0, s_a, cm_a))

        def pair(jj, carry):
            j = 2 * jj
            phase((j + 1, s_b, cm_b), set_a, (jnp.maximum(j - 1, 0), p_b, al_b))
            phase((j + 2, s_a, cm_a), set_b, (j, p_a, al_a))
            return carry

        lax.fori_loop(0, qi - 1, pair, 0)
        d0 = 2 * qi
        phase((d0 - 1, s_b, cm_b), set_a, (jnp.maximum(d0 - 3, 0), p_b, al_b))
        phase((d0, s_a, cm_a, 0), set_b, (d0 - 2, p_a, al_a))
        phase((d0 + 1, s_b, cm_b, 1), set_a, (d0 - 1, p_b, al_b))
        phase(None, set_b, (d0, p_a, al_a))
        phase(None, None, (d0 + 1, p_b, al_b))

    for c in range(tq // ATTN_COLS):
        out_t = acc_ref[c, :V_HEAD_DIM, :] / acc_ref[c, V_HEAD_DIM:V_HEAD_DIM + 1, :]
        o_ref[c * ATTN_COLS:(c + 1) * ATTN_COLS, :] = jnp.transpose(out_t).astype(o_ref.dtype)


def _flash_attention(qt, k, vt):
    b, h, _, s = qt.shape
    tq, tk = ATTN_Q_TILE, ATTN_K_TILE
    assert vt.shape == (b, h, s // tk, V_HEAD_DIM + DENOM_ROWS, tk)
    score_buf = pltpu.VMEM((tq // ATTN_COLS, tk, ATTN_COLS), F32)
    prob_buf = pltpu.VMEM((tq // ATTN_COLS, tk, ATTN_COLS), BF16)
    stat = pltpu.VMEM((1, tq), F32)
    return pl.pallas_call(
        functools.partial(_flash_kernel, tq=tq, tk=tk),
        grid=(b, h, s // tq),
        in_specs=[pl.BlockSpec((None, None, QK_DIM, tq), lambda i, j, q: (i, j, 0, q)),
                  pl.BlockSpec((None, None, s, QK_DIM), lambda i, j, q: (i, j, 0, 0)),
                  pl.BlockSpec((None, None, s // tk, V_HEAD_DIM + DENOM_ROWS, tk), lambda i, j, q: (i, j, 0, 0, 0))],
        out_specs=pl.BlockSpec((None, tq, V_HEAD_DIM), lambda i, j, q: (i, q, j)),
        out_shape=jax.ShapeDtypeStruct((b, s, h * V_HEAD_DIM), BF16),
        scratch_shapes=[score_buf, score_buf, prob_buf, prob_buf, stat, stat, stat, stat, stat,
                        pltpu.VMEM((tq // ATTN_COLS, V_HEAD_DIM + DENOM_ROWS, ATTN_COLS), F32)],
        compiler_params=pltpu.CompilerParams(dimension_semantics=("parallel", "parallel", "arbitrary"),
                                             vmem_limit_bytes=48 * MIB),
        name="flash_attention",
    )(qt, k, vt)


def _lru_kernel(x_ref, g_ref, perm_ref, perm_t_ref, w_in_ref, conv_w_ref, conv_b_ref, w_gate_ref,
                b_a_ref, b_x_ref, lam_ref, o_ref, tail_ref, h_ref, out_ref, *, tm):
    seg = tm // SUBLANES
    first_row = lax.broadcasted_iota(jnp.int32, (SUBLANES, LRU_BLOCK_SIZE), 0) == 0
    first_row_wide = lax.broadcasted_iota(jnp.int32, (SUBLANES, LRU_WIDTH), 0) == 0

    @pl.when(pl.program_id(1) == 0)
    def _():
        tail_ref[...] = jnp.zeros(tail_ref.shape, F32)
        h_ref[...] = jnp.zeros(h_ref.shape, F32)

    hn = _rms(x_ref[...], g_ref[...]).astype(BF16)
    hn = _dot(perm_ref[...], hn).astype(BF16)
    proj = _dot(hn, w_in_ref[...])
    gate = jax.nn.gelu(proj[:, :LRU_WIDTH], approximate=True)
    rec = proj[:, LRU_WIDTH:]

    wrapped = []
    for m in range(CONV_WIDTH - 1):
        row0 = (seg - (CONV_WIDTH - 1) + m) * SUBLANES
        cur = pltpu.roll(rec[row0:row0 + SUBLANES, :], 1, axis=0)
        prev = pltpu.roll(tail_ref[m * SUBLANES:(m + 1) * SUBLANES, :], 1, axis=0)
        wrapped.append(jnp.where(first_row_wide, prev, cur))
    conv = rec * conv_w_ref[CONV_WIDTH - 1:CONV_WIDTH, :]
    for k in range(1, CONV_WIDTH):
        shifted = jnp.concatenate(wrapped[CONV_WIDTH - 1 - k:] + [rec[:(seg - k) * SUBLANES, :]], axis=0)
        conv = conv + shifted * conv_w_ref[CONV_WIDTH - 1 - k:CONV_WIDTH - k, :]
    xc = conv_b_ref[...] + conv
    tail_ref[...] = rec[(seg - (CONV_WIDTH - 1)) * SUBLANES:, :]

    xc_bf = xc.astype(BF16)
    neg_c_softplus = -LRU_C * jax.nn.softplus(-lam_ref[...])
    for n in range(LRU_BLOCKS):
        lanes = slice(n * LRU_BLOCK_SIZE, (n + 1) * LRU_BLOCK_SIZE)
        gates = _dot(xc_bf[:, lanes], w_gate_ref[n])
        r = jax.nn.sigmoid(gates[:, :LRU_BLOCK_SIZE] + b_a_ref[:, lanes])
        i = jax.nn.sigmoid(gates[:, LRU_BLOCK_SIZE:] + b_x_ref[:, lanes])
        log_a = neg_c_softplus[:, lanes] * r
        a = jnp.exp(log_a)
        one_minus_a2 = jnp.tanh(-log_a) * (a * a + 1.0)
        b = jnp.sqrt(one_minus_a2) * (i * xc[:, lanes])

        local = jnp.zeros((SUBLANES, LRU_BLOCK_SIZE), F32)
        decay = jnp.ones((SUBLANES, LRU_BLOCK_SIZE), F32)
        locals_, decays = [], []
        for j in range(seg):
            rows = slice(j * SUBLANES, (j + 1) * SUBLANES)
            local = a[rows, :] * local + b[rows, :]
            decay = a[rows, :] * decay
            locals_.append(local)
            decays.append(decay)

        carry_in = pltpu.roll(h_ref[:, lanes], 1, axis=0)
        h_in = carry_in
        for _ in range(SUBLANES - 1):
            h_in = jnp.where(first_row, carry_in, pltpu.roll(local + decay * h_in, 1, axis=0))
        h_ref[:, lanes] = local + decay * h_in

        out = [gate[j * SUBLANES:(j + 1) * SUBLANES, lanes] * (locals_[j] + decays[j] * h_in)
               for j in range(seg)]
        out_ref[:, lanes] = jnp.concatenate(out, axis=0).astype(BF16)

    o_ref[...] = _dot(perm_t_ref[...], out_ref[...]).astype(o_ref.dtype)


def _lru_mixer(x, g_pre, w_in, conv_w, conv_b, w_gate, b_a, b_x, lam):
    b, s, d = x.shape
    tm = ROW_TILE
    row = lambda i, j: (i, j, 0)
    idx = jnp.arange(tm)
    perm = ((tm // SUBLANES) * (idx % SUBLANES) + idx // SUBLANES)[:, None] == idx[None, :]
    perm = perm.astype(BF16)
    return pl.pallas_call(
        functools.partial(_lru_kernel, tm=tm),
        grid=(b, s // tm),
        in_specs=[pl.BlockSpec((None, tm, d), row),
                  _resident(g_pre.shape), _resident(perm.shape), _resident(perm.shape),
                  _resident(w_in.shape), _resident(conv_w.shape),
                  _resident(conv_b.shape), _resident(w_gate.shape), _resident(b_a.shape),
                  _resident(b_x.shape), _resident(lam.shape)],
        out_specs=pl.BlockSpec((None, tm, LRU_WIDTH), row),
        out_shape=jax.ShapeDtypeStruct((b, s, LRU_WIDTH), BF16),
        scratch_shapes=[pltpu.VMEM(((CONV_WIDTH - 1) * SUBLANES, LRU_WIDTH), F32),
                        pltpu.VMEM((SUBLANES, LRU_WIDTH), F32),
                        pltpu.VMEM((tm, LRU_WIDTH), BF16)],
        compiler_params=pltpu.CompilerParams(dimension_semantics=("arbitrary", "arbitrary"),
                                             vmem_limit_bytes=48 * MIB),
        name="lru_mixer",
    )(x, g_pre, perm, perm.T, w_in, conv_w, conv_b, w_gate, b_a, b_x, lam)


def _post_ffn_kernel(x_ref, t_ref, w_out_ref, g_post_ref, g_pre_ref, w_up_ref, w_down_ref, g_ffn_ref,
                     o_ref):
    y = _dot(t_ref[...], w_out_ref[...])
    x1 = x_ref[...] + _rms(y, g_post_ref[...])
    hn = _rms(x1, g_pre_ref[...]).astype(BF16)
    acc = None
    for c in range(FFN_HIDDEN // FFN_CHUNK):
        cols = slice(c * FFN_CHUNK, (c + 1) * FFN_CHUNK)
        u = jnp.maximum(_dot(hn, w_up_ref[:, cols]), 0.0)
        part = _dot((u * u).astype(BF16), w_down_ref[cols, :])
        acc = part if acc is None else acc + part
    o_ref[...] = x1 + _rms(acc, g_ffn_ref[...])


def _post_ffn(x, t, w_out, g_post, g_pre, w_up, w_down, g_ffn):
    n, d = x.shape
    tm = ROW_TILE
    row = lambda i: (i, 0)
    return pl.pallas_call(
        _post_ffn_kernel,
        grid=(n // tm,),
        in_specs=[pl.BlockSpec((tm, d), row), pl.BlockSpec((tm, t.shape[1]), row),
                  _resident(w_out.shape), _resident(g_post.shape), _resident(g_pre.shape),
                  _resident(w_up.shape), _resident(w_down.shape), _resident(g_ffn.shape)],
        out_specs=pl.BlockSpec((tm, d), row),
        out_shape=jax.ShapeDtypeStruct((n, d), F32),
        compiler_params=pltpu.CompilerParams(dimension_semantics=("parallel",),
                                             vmem_limit_bytes=56 * MIB),
        name="post_ffn",
    )(x, t, w_out, g_post, g_pre, w_up, w_down, g_ffn)


def _swap_halves(w):
    half = w.shape[-1] // 2
    return jnp.concatenate([w[..., half:], w[..., :half]], axis=-1)


def _mla_weights(w_in, w_uq, w_ukv):
    kv_end = Q_LORA_RANK + KV_LORA_RANK
    w_in_ext = jnp.concatenate([w_in, _swap_halves(w_in[:, kv_end:])], axis=1).astype(BF16)
    w_uqt = w_uq.T.astype(BF16)
    ukv = w_ukv.reshape(KV_LORA_RANK, MLA_HEADS, QK_NOPE_DIM + V_HEAD_DIM)
    w_kn = ukv[..., :QK_NOPE_DIM].reshape(KV_LORA_RANK, MLA_HEADS * QK_NOPE_DIM).astype(BF16)
    w_vt = ukv[..., QK_NOPE_DIM:].reshape(KV_LORA_RANK, MLA_HEADS * V_HEAD_DIM).T.astype(BF16)
    return w_in_ext, w_uqt, w_kn, w_vt


def kernel(x, positions, mix_pre_norm, mix_post_norm, ffn_pre_norm, ffn_post_norm, mla_w_in, mla_q_norm, mla_kv_norm, mla_w_uq, mla_w_ukv, mla_w_o, lru_w_in, lru_conv_w, lru_conv_b, lru_w_a, lru_b_a, lru_w_x, lru_b_x, lru_lambda, lru_w_out, ffn_w_up, ffn_w_down):
    b, s, d = x.shape
    depth = mix_pre_norm.shape[0]
    vec = lambda v: v.reshape(1, -1)

    inv_freq = ROPE_THETA ** (-jnp.arange(0, QK_ROPE_DIM, 2, dtype=F32) / QK_ROPE_DIM)
    cct, sst = _rope_tables(positions, inv_freq)
    cc, ss = jnp.swapaxes(cct, 1, 2), jnp.swapaxes(sst, 1, 2)

    for layer in range(depth):
        j = layer // 2
        g_pre = vec(mix_pre_norm[layer])
        if layer % 2 == 0:
            w_in_ext, w_uqt, w_kn, w_vt = _mla_weights(mla_w_in[j], mla_w_uq[j], mla_w_ukv[j])
            qt, k, vt = _mla_proj(x, g_pre, w_in_ext, vec(mla_q_norm[j]), vec(mla_kv_norm[j]),
                                  w_uqt, w_kn, w_vt, cct, sst, cc, ss)
            t = _flash_attention(qt, k, vt)
            w_out = mla_w_o[j]
        else:
            w_gate = jnp.concatenate([lru_w_a[j], lru_w_x[j]], axis=-1).astype(BF16)
            t = _lru_mixer(x, g_pre, lru_w_in[j].astype(BF16), lru_conv_w[j], vec(lru_conv_b[j]), w_gate,
                           vec(lru_b_a[j]), vec(lru_b_x[j]), vec(lru_lambda[j]))
            w_out = lru_w_out[j]
        x = _post_ffn(x.reshape(b * s, d), t.reshape(b * s, -1), w_out.astype(BF16),
                      vec(mix_post_norm[layer]), vec(ffn_pre_norm[layer]),
                      ffn_w_up[layer].astype(BF16), ffn_w_down[layer].astype(BF16),
                      vec(ffn_post_norm[layer])).reshape(b, s, d)
    return x
```
